```python
import jax, jax.numpy as jnp
from jax import lax
import numpy as np

D_MODEL = 1024
BATCH = 16
SEQ = 4096
DEPTH = 1

D_MIX = D_MODEL
HG_WIDTH = D_MIX // 2
HG_HEADS = 4
HG_DK = HG_WIDTH // HG_HEADS
HG_DV = HG_WIDTH // HG_HEADS
HGRN_CHUNK = 16
GM_WIDTH = D_MIX - HG_WIDTH
GM_GROUPS = 4
GM_GROUP_DIM = GM_WIDTH // GM_GROUPS
GM_CHUNK = 128
PROJ_SIZES = (HG_WIDTH, HG_WIDTH, HG_WIDTH, HG_WIDTH, HG_WIDTH, GM_WIDTH, GM_WIDTH)
PROJ_TOTAL = 5 * HG_WIDTH + 2 * GM_WIDTH
PEER_HEADS = 8
PEER_N_KEYS = 128
PEER_N_EXPERTS = PEER_N_KEYS * PEER_N_KEYS
PEER_TOPK = 16
PEER_QUERY_DIM = 256
PEER_HALF = PEER_QUERY_DIM // 2
PEER_BLOCK = 128
DEEPNORM_ALPHA = (2 * DEPTH) ** 0.25
DEEPNORM_BETA = (8 * DEPTH) ** -0.25
LN_EPS = 1e-5
RMS_EPS = 1e-6

kernel_name = "hybrid_hgrn2_gmlp_peer_deepnorm"


def _layer_norm(x, g, b):
    xf = x.astype(jnp.float32)
    mu = jnp.mean(xf, axis=-1, keepdims=True)
    var = jnp.mean(jnp.square(xf - mu), axis=-1, keepdims=True)
    y = (xf - mu) * lax.rsqrt(var + LN_EPS) * g.astype(jnp.float32) + b.astype(jnp.float32)
    return y.astype(x.dtype)


def _gla_chunkwise(q, k, v, logf):
    B, H, S, DK = q.shape
    DV = v.shape[-1]
    n = S // HGRN_CHUNK
    q = q.reshape(B, H, n, HGRN_CHUNK, DK)
    k = k.reshape(B, H, n, HGRN_CHUNK, DK)
    v = v.reshape(B, H, n, HGRN_CHUNK, DV)
    logf = logf.reshape(B, H, n, HGRN_CHUNK, DK)
    b = jnp.cumsum(logf, axis=3)
    b_last = b[:, :, :, -1:, :]
    q_rel = q * jnp.exp(b - b_last)
    k_rel = k * jnp.exp(b_last - b)
    scores = jnp.einsum('bhntd,bhnsd->bhnts', q_rel, k_rel)
    tri = jnp.tril(jnp.ones((HGRN_CHUNK, HGRN_CHUNK), dtype=bool))
    o_intra = jnp.einsum('bhnts,bhnse->bhnte', jnp.where(tri, scores, 0.0), v)
    q_inter = q * jnp.exp(b)
    decay = jnp.exp(b_last[:, :, :, 0, :])

    def step(state, xs):
        qi, ki, vi, dc = xs
        o = jnp.einsum('bhtd,bhde->bhte', qi, state)
        state = state * dc[..., None] + jnp.einsum('bhsd,bhse->bhde', ki, vi)
        return state, o

    xs = (jnp.moveaxis(q_inter, 2, 0), jnp.moveaxis(k_rel, 2, 0),
          jnp.moveaxis(v, 2, 0), jnp.moveaxis(decay, 2, 0))
    state0 = jnp.zeros((B, H, DK, DV), q.dtype)
    _, o_inter = lax.scan(step, state0, xs)
    o = o_intra + jnp.moveaxis(o_inter, 0, 2)
    return o.reshape(B, H, S, DV)


def _hybrid_mixer(x, w_in, lb, hg_norm_g, gm_ln_g, gm_ln_b, gm_ws, gm_bs, w_out):
    B, S, _ = x.shape
    proj = jnp.einsum('bsd,de->bse', x, w_in)
    splits = [int(c) for c in np.cumsum(PROJ_SIZES)[:-1]]
    q, z_f, z_b, i_in, g, u, v = jnp.split(proj, splits, axis=-1)

    def heads(t):
        return t.astype(jnp.float32).reshape(B, S, HG_HEADS, -1).transpose(0, 2, 1, 3)

    qh, vh = heads(q), heads(i_in)

    def forget(z, lb_dir):
        z = heads(z)
        lb_dir = lb_dir.reshape(HG_HEADS, 1, HG_DK)
        logf = jnp.log(lb_dir + (1.0 - lb_dir) * jax.nn.sigmoid(z))
        k = (1.0 - lb_dir) * jax.nn.sigmoid(-z)
        return k, logf

    k_f, logf_f = forget(z_f, lb[0])
    k_b, logf_b = forget(z_b, lb[1])
    o_f = _gla_chunkwise(qh, k_f, vh, logf_f)
    flip = lambda t: jnp.flip(t, axis=2)
    o_b = flip(_gla_chunkwise(flip(qh), flip(k_b), flip(vh), flip(logf_b)))
    o = o_f + o_b
    o = o * lax.rsqrt(jnp.mean(jnp.square(o), axis=-1, keepdims=True) + RMS_EPS)
    o = o.transpose(0, 2, 1, 3).reshape(B, S, HG_WIDTH)
    y_hg = (o * hg_norm_g.astype(jnp.float32) * jax.nn.silu(g.astype(jnp.float32))).astype(x.dtype)

    u = jax.nn.gelu(u)
    v = jax.nn.gelu(v)
    n = S // GM_CHUNK
    vc = v.reshape(B, n, GM_CHUNK, GM_GROUPS, GM_GROUP_DIM)
    vc = _layer_norm(vc, gm_ln_g, gm_ln_b)
    sp = jnp.einsum('gpq,bnqgc->bnpgc', gm_ws, vc) + gm_bs.T[:, :, None]
    y_gm = u * sp.reshape(B, S, GM_WIDTH)

    y = jnp.concatenate([y_hg, y_gm], axis=-1)
    return jnp.einsum('bse,ed->bsd', y, w_out)


def _peer(x, wq, k1, k2, u_tab, v_tab):
    B, S, D = x.shape
    blocks = x.reshape(-1, PEER_BLOCK, D)

    def one_block(xb):
        T = xb.shape[0]
        qb = jnp.einsum('td,de->te', xb, wq).reshape(T, PEER_HEADS, 2, PEER_HALF)
        s1 = jnp.einsum('thd,hkd->thk', qb[:, :, 0], k1)
        s2 = jnp.einsum('thd,hkd->thk', qb[:, :, 1], k2)
        v1, i1 = lax.top_k(s1, PEER_TOPK)
        v2, i2 = lax.top_k(s2, PEER_TOPK)
        cand = (v1[..., :, None] + v2[..., None, :]).reshape(T, PEER_HEADS, PEER_TOPK * PEER_TOPK)
        cand_idx = (i1[..., :, None] * PEER_N_KEYS + i2[..., None, :]).reshape(T, PEER_HEADS, PEER_TOPK * PEER_TOPK)
        top_s, pos = lax.top_k(cand, PEER_TOPK)
        expert = jnp.take_along_axis(cand_idx, pos, axis=-1).reshape(T, PEER_HEADS * PEER_TOPK)
        gate = jax.nn.softmax(top_s.astype(jnp.float32), axis=-1).astype(xb.dtype)
        gate = gate.reshape(T, PEER_HEADS * PEER_TOPK)
        hid = jnp.einsum('td,tkd->tk', xb, u_tab[expert])
        act = gate * jax.nn.gelu(hid)
        return jnp.einsum('tk,tkd->td', act, v_tab[expert])

    out = lax.map(one_block, blocks)
    return out.reshape(B, S, D)


def setup_inputs(seed: int = 0) -> dict:
    key = jax.random.key(seed)
    ks = jax.random.split(key, 20)
    f32 = jnp.float32
    nrm = lambda k, shape, s: jax.random.normal(k, shape, f32) * s
    x = jax.random.normal(ks[0], (BATCH, SEQ, D_MODEL), f32)
    col_scale = jnp.concatenate([
        jnp.ones((3 * HG_WIDTH,), f32), jnp.full((HG_WIDTH,), DEEPNORM_BETA, f32),
        jnp.ones((HG_WIDTH,), f32), jnp.full((GM_WIDTH,), DEEPNORM_BETA, f32),
        jnp.ones((GM_WIDTH,), f32)])
    w_in = nrm(ks[1], (DEPTH, D_MODEL, PROJ_TOTAL), D_MODEL ** -0.5) * col_scale
    hgrn_lb_logits = nrm(ks[2], (2, DEPTH + 1, HG_WIDTH), 0.1)
    hgrn_norm_g = 1.0 + nrm(ks[3], (DEPTH, HG_WIDTH), 0.02)
    gmlp_ln_g = 1.0 + nrm(ks[4], (DEPTH, GM_GROUPS, GM_GROUP_DIM), 0.02)
    gmlp_ln_b = nrm(ks[5], (DEPTH, GM_GROUPS, GM_GROUP_DIM), 0.02)
    gmlp_ws = nrm(ks[6], (DEPTH, GM_GROUPS, GM_CHUNK, GM_CHUNK), GM_CHUNK ** -0.5)
    gmlp_bs = 1.0 + nrm(ks[7], (DEPTH, GM_GROUPS, GM_CHUNK), 0.02)
    w_out = nrm(ks[8], (DEPTH, D_MIX, D_MODEL), DEEPNORM_BETA * D_MIX ** -0.5)
    ln1_g = 1.0 + nrm(ks[9], (DEPTH, D_MODEL), 0.02)
    ln1_b = nrm(ks[10], (DEPTH, D_MODEL), 0.02)
    peer_wq = nrm(ks[11], (DEPTH, D_MODEL, PEER_HEADS * PEER_QUERY_DIM), D_MODEL ** -0.5)
    peer_k1 = nrm(ks[12], (DEPTH, PEER_HEADS, PEER_N_KEYS, PEER_HALF), PEER_HALF ** -0.5)
    peer_k2 = nrm(ks[13], (DEPTH, PEER_HEADS, PEER_N_KEYS, PEER_HALF), PEER_HALF ** -0.5)
    peer_u = nrm(ks[14], (DEPTH, PEER_N_EXPERTS, D_MODEL), D_MODEL ** -0.5)
    peer_v = nrm(ks[15], (DEPTH, PEER_N_EXPERTS, D_MODEL), DEEPNORM_BETA * (PEER_HEADS * PEER_TOPK) ** -0.5)
    ln2_g = 1.0 + nrm(ks[16], (DEPTH, D_MODEL), 0.02)
    ln2_b = nrm(ks[17], (DEPTH, D_MODEL), 0.02)
    return {"x": x, "w_in": w_in, "hgrn_lb_logits": hgrn_lb_logits, "hgrn_norm_g": hgrn_norm_g,
            "gmlp_ln_g": gmlp_ln_g, "gmlp_ln_b": gmlp_ln_b, "gmlp_ws": gmlp_ws, "gmlp_bs": gmlp_bs,
            "w_out": w_out, "ln1_g": ln1_g, "ln1_b": ln1_b, "peer_wq": peer_wq,
            "peer_k1": peer_k1, "peer_k2": peer_k2, "peer_u": peer_u, "peer_v": peer_v,
            "ln2_g": ln2_g, "ln2_b": ln2_b}


def reference(x, w_in, hgrn_lb_logits, hgrn_norm_g, gmlp_ln_g, gmlp_ln_b, gmlp_ws, gmlp_bs,
              w_out, ln1_g, ln1_b, peer_wq, peer_k1, peer_k2, peer_u, peer_v, ln2_g, ln2_b):
    lb_all = jnp.cumsum(jax.nn.softmax(hgrn_lb_logits.astype(jnp.float32), axis=1), axis=1)
    h = x
    for l in range(DEPTH):
        mix = _hybrid_mixer(h, w_in[l], lb_all[:, l], hgrn_norm_g[l], gmlp_ln_g[l], gmlp_ln_b[l],
                            gmlp_ws[l], gmlp_bs[l], w_out[l])
        h = _layer_norm(DEEPNORM_ALPHA * h + mix, ln1_g[l], ln1_b[l])
        ffn = _peer(h, peer_wq[l], peer_k1[l], peer_k2[l], peer_u[l], peer_v[l])
        h = _layer_norm(DEEPNORM_ALPHA * h + ffn, ln2_g[l], ln2_b[l])
    return h
```

```python
import functools

import jax
import jax.numpy as jnp
from jax import lax
from jax.experimental import pallas as pl
from jax.experimental.pallas import tpu as pltpu

F32 = jnp.float32
BF16 = jnp.bfloat16

LANES = 128
SUBLANES = 8
VMEM_BYTES = 64 * 1024 * 1024

HG_HEADS = 4
HEAD_DIM = 128
HG_CHUNK = 128
HG_SUB = 16
GM_GROUPS = 4
GM_DIM = 128
GM_CHUNK = 128
PEER_HEADS = 8
PEER_KEYS = 128
PEER_TOPK = 16
PEER_HALF = 128
PEER_SLOTS = PEER_HEADS * PEER_TOPK
LN_EPS = 1e-5
RMS_EPS = 1e-6

ROW_BLOCK = 512
ROUTE_BLOCK = 256
GATHER_BLOCK = 128


def _vmem_limit(nbytes):
    return int(min(max(2 * nbytes, 32 * 1024 * 1024), VMEM_BYTES - 6 * 1024 * 1024))


def _dot(a, b):
    return jnp.dot(a, b, preferred_element_type=F32)


def _dot_nt(a, b):
    return lax.dot_general(a, b, (((1,), (1,)), ((), ())), preferred_element_type=F32)


def _dot_tn(a, b):
    return lax.dot_general(a, b, (((0,), (0,)), ((), ())), preferred_element_type=F32)


def _layer_norm(x, g, b):
    mu = jnp.mean(x, axis=-1, keepdims=True)
    xc = x - mu
    var = jnp.mean(xc * xc, axis=-1, keepdims=True)
    return xc * lax.rsqrt(var + LN_EPS) * g + b


def _proj_kernel(x_ref, w_ref, o_ref):
    o_ref[...] = _dot(x_ref[...].astype(BF16), w_ref[...])


def _proj(x2d, w_bf16):
    n, d = x2d.shape
    e = w_bf16.shape[1]
    est = 2 * ROW_BLOCK * d * 4 + d * e * 2 + 2 * ROW_BLOCK * e * 4
    return pl.pallas_call(
        _proj_kernel,
        grid=(n // ROW_BLOCK,),
        in_specs=[pl.BlockSpec((ROW_BLOCK, d), lambda i: (i, 0)),
                  pl.BlockSpec((d, e), lambda i: (0, 0), pipeline_mode=pl.Buffered(1))],
        out_specs=pl.BlockSpec((ROW_BLOCK, e), lambda i: (i, 0)),
        out_shape=jax.ShapeDtypeStruct((n, e), F32),
        compiler_params=pltpu.CompilerParams(dimension_semantics=("parallel",),
                                             vmem_limit_bytes=_vmem_limit(est)),
        name="proj",
    )(x2d, w_bf16)


def _cumsum_rows(x, reverse):
    n = x.shape[0]
    row = lax.broadcasted_iota(jnp.int32, x.shape, 0)
    sh = 1
    while sh < n:
        if reverse:
            x = x + jnp.where(row < n - sh, pltpu.roll(x, n - sh, axis=0), 0.0)
        else:
            x = x + jnp.where(row >= sh, pltpu.roll(x, sh, axis=0), 0.0)
        sh *= 2
    return x


def _gla_chunk(q, z, v, lb, st, reverse):
    c = q.shape[0]
    one_m_lb = 1.0 - lb
    logf = jnp.log(lb + one_m_lb * jax.nn.sigmoid(z))
    k = one_m_lb * jax.nn.sigmoid(-z)
    b = _cumsum_rows(logf, reverse)
    b_edge = b[0:1] if reverse else b[c - 1:c]
    o = _dot_nt((q * jnp.exp(b)).astype(BF16), st.astype(BF16))
    k_rel = k * jnp.exp(b_edge - b)
    vb = v.astype(BF16)
    st_new = st * jnp.exp(b_edge) + _dot_tn(vb, k_rel.astype(BF16))
    pieces = []
    for j in range(c // HG_SUB):
        r0 = j * HG_SUB
        if reverse:
            ref_row, lo, hi = b[r0:r0 + 1], 0, r0 + HG_SUB
        else:
            ref_row, lo, hi = b[r0 + HG_SUB - 1:r0 + HG_SUB], r0, c
        kj = k[r0:r0 + HG_SUB] * jnp.exp(ref_row - b[r0:r0 + HG_SUB])
        qj = q[lo:hi] * jnp.exp(b[lo:hi] - ref_row)
        a = _dot_nt(qj.astype(BF16), kj.astype(BF16))
        t_abs = lo + lax.broadcasted_iota(jnp.int32, a.shape, 0)
        s_abs = r0 + lax.broadcasted_iota(jnp.int32, a.shape, 1)
        keep = (s_abs >= t_abs) if reverse else (s_abs <= t_abs)
        a = jnp.where(keep, a, 0.0)
        oj = _dot(a.astype(BF16), vb[r0:r0 + HG_SUB])
        parts = []
        if lo > 0:
            parts.append(jnp.zeros((lo, HEAD_DIM), F32))
        parts.append(oj)
        if hi < c:
            parts.append(jnp.zeros((c - hi, HEAD_DIM), F32))
        pieces.append(parts[0] if len(parts) == 1 else jnp.concatenate(parts, axis=0))
    for p in pieces:
        o = o + p
    return o, st_new


def _hgrn_kernel(layer, q_ref, zf_ref, zb_ref, v_ref, g_ref, lbl_ref, ng_ref, y_ref, of_ref):
    s = q_ref.shape[0]
    nch = s // HG_CHUNK

    def lower_bound(direction):
        lg = [lbl_ref[direction, m] for m in range(lbl_ref.shape[1])]
        mx = functools.reduce(jnp.maximum, lg)
        ex = [jnp.exp(t - mx) for t in lg]
        return functools.reduce(jnp.add, ex[:layer + 1]) / functools.reduce(jnp.add, ex)

    lb_f = lower_bound(0)
    lb_b = lower_bound(1)
    st0 = jnp.zeros((HEAD_DIM, HEAD_DIM), F32)

    def fwd(n, st):
        rows = pl.ds(pl.multiple_of(n * HG_CHUNK, HG_CHUNK), HG_CHUNK)
        o, st = _gla_chunk(q_ref[rows, :], zf_ref[rows, :], v_ref[rows, :], lb_f, st, False)
        of_ref[rows, :] = o
        return st

    lax.fori_loop(0, nch, fwd, st0)

    def bwd(i, st):
        n = nch - 1 - i
        rows = pl.ds(pl.multiple_of(n * HG_CHUNK, HG_CHUNK), HG_CHUNK)
        o, st = _gla_chunk(q_ref[rows, :], zb_ref[rows, :], v_ref[rows, :], lb_b, st, True)
        o = o + of_ref[rows, :]
        o = o * lax.rsqrt(jnp.mean(o * o, axis=-1, keepdims=True) + RMS_EPS)
        y_ref[rows, :] = o * ng_ref[...] * jax.nn.silu(g_ref[rows, :])
        return st

    lax.fori_loop(0, nch, bwd, st0)


def _hgrn(proj3, lb_logits, norm_g, layer):
    bsz, s, _ = proj3.shape
    hw = HG_HEADS * HEAD_DIM
    nslot = lb_logits.shape[1]
    lbl = lb_logits.astype(F32).reshape(2, nslot, HG_HEADS, 1, HEAD_DIM)
    ng = norm_g.astype(F32).reshape(HG_HEADS, 1, HEAD_DIM)

    def col(c):
        return pl.BlockSpec((None, s, HEAD_DIM), lambda b, h, c=c: (b, 0, c * HG_HEADS + h))

    est = (2 * 5 + 2 + 1) * s * HEAD_DIM * 4
    return pl.pallas_call(
        functools.partial(_hgrn_kernel, layer),
        grid=(bsz, HG_HEADS),
        in_specs=[col(0), col(1), col(2), col(3), col(4),
                  pl.BlockSpec((2, nslot, None, 1, HEAD_DIM), lambda b, h: (0, 0, h, 0, 0)),
                  pl.BlockSpec((None, 1, HEAD_DIM), lambda b, h: (h, 0, 0))],
        out_specs=pl.BlockSpec((None, s, HEAD_DIM), lambda b, h: (b, 0, h)),
        out_shape=jax.ShapeDtypeStruct((bsz, s, hw), F32),
        scratch_shapes=[pltpu.VMEM((s, HEAD_DIM), F32)],
        compiler_params=pltpu.CompilerParams(dimension_semantics=("parallel", "parallel"),
                                             vmem_limit_bytes=_vmem_limit(est)),
        name="hgrn",
    )(proj3, proj3, proj3, proj3, proj3, lbl, ng)


def _mix_kernel(alpha, yhg_ref, u_ref, v_ref, x_ref, lng_ref, lnb_ref, ws_ref, bs_ref, wo_ref,
                g1_ref, b1_ref, h_ref, y_ref):
    rows = u_ref.shape[0]
    hw = yhg_ref.shape[1]
    y_ref[:, 0:hw] = yhg_ref[...].astype(BF16)
    for n in range(rows // GM_CHUNK):
        r = slice(n * GM_CHUNK, (n + 1) * GM_CHUNK)
        for g in range(GM_GROUPS):
            cs = slice(g * GM_DIM, (g + 1) * GM_DIM)
            vg = jax.nn.gelu(v_ref[r, cs])
            vn = _layer_norm(vg, lng_ref[:, cs], lnb_ref[:, cs])
            sp = _dot(ws_ref[g], vn.astype(BF16)) + bs_ref[g]
            y_ref[r, hw + g * GM_DIM:hw + (g + 1) * GM_DIM] = (jax.nn.gelu(u_ref[r, cs]) * sp).astype(BF16)
    mix = _dot(y_ref[...], wo_ref[...])
    h_ref[...] = _layer_norm(alpha * x_ref[...] + mix, g1_ref[...], b1_ref[...])


def _mix(yhg, proj3, x3, gm_ln_g, gm_ln_b, gm_ws, gm_bs, w_out, ln_g, ln_b, alpha):
    bsz, s, d = x3.shape
    hw = yhg.shape[2]
    gw = GM_GROUPS * GM_DIM
    dmix = hw + gw
    ucol = (proj3.shape[2] - 2 * gw) // gw
    rb = ROW_BLOCK
    row = lambda c: (lambda b, i: (b, i, c))
    full2 = lambda b, i: (0, 0)
    full3 = lambda b, i: (0, 0, 0)
    bs_b = jnp.broadcast_to(gm_bs.astype(F32)[:, :, None], (GM_GROUPS, GM_CHUNK, GM_DIM))
    est = 2 * rb * (hw + 2 * gw + 2 * d) * 4 + dmix * d * 2 + rb * dmix * 2
    return pl.pallas_call(
        functools.partial(_mix_kernel, alpha),
        grid=(bsz, s // rb),
        in_specs=[pl.BlockSpec((None, rb, hw), row(0)),
                  pl.BlockSpec((None, rb, gw), row(ucol)),
                  pl.BlockSpec((None, rb, gw), row(ucol + 1)),
                  pl.BlockSpec((None, rb, d), row(0)),
                  pl.BlockSpec((1, gw), full2), pl.BlockSpec((1, gw), full2),
                  pl.BlockSpec((GM_GROUPS, GM_CHUNK, GM_CHUNK), full3),
                  pl.BlockSpec((GM_GROUPS, GM_CHUNK, GM_DIM), full3),
                  pl.BlockSpec((dmix, d), full2),
                  pl.BlockSpec((1, d), full2), pl.BlockSpec((1, d), full2)],
        out_specs=pl.BlockSpec((None, rb, d), row(0)),
        out_shape=jax.ShapeDtypeStruct((bsz, s, d), F32),
        scratch_shapes=[pltpu.VMEM((rb, dmix), BF16)],
        compiler_params=pltpu.CompilerParams(dimension_semantics=("parallel", "parallel"),
                                             vmem_limit_bytes=_vmem_limit(est)),
        name="mix",
    )(yhg, proj3, proj3, x3, gm_ln_g.astype(F32).reshape(1, gw), gm_ln_b.astype(F32).reshape(1, gw),
      gm_ws.astype(BF16), bs_b, w_out.astype(BF16), ln_g.astype(F32).reshape(1, d), ln_b.astype(F32).reshape(1, d))


def _topk_rows(s, payload):
    nrow = s.shape[0]
    rid = lax.broadcasted_iota(jnp.int32, s.shape, 0)
    big = jnp.iinfo(jnp.int32).max
    vals, picks = [], []
    for _ in range(PEER_TOPK):
        m = jnp.max(s, axis=0, keepdims=True)
        am = jnp.min(jnp.where(s == m, rid, nrow), axis=0, keepdims=True)
        sel = rid == am
        vals.append(m)
        picks.append(am if payload is None else jnp.min(jnp.where(sel, payload, big), axis=0, keepdims=True))
        s = jnp.where(sel, -jnp.inf, s)
    return jnp.concatenate(vals, axis=0), jnp.concatenate(picks, axis=0)


def _route_kernel(h_ref, wq_ref, k1_ref, k2_ref, idx_ref, gate_ref):
    tb = h_ref.shape[0]
    qb = _dot(h_ref[...].astype(BF16), wq_ref[...]).astype(BF16)
    idx_rows, gate_rows = [], []
    for h in range(PEER_HEADS):
        c0 = h * 2 * PEER_HALF
        s1 = _dot_nt(k1_ref[h], qb[:, c0:c0 + PEER_HALF])
        s2 = _dot_nt(k2_ref[h], qb[:, c0 + PEER_HALF:c0 + 2 * PEER_HALF])
        v1, i1 = _topk_rows(s1, None)
        v2, i2 = _topk_rows(s2, None)
        cand = jnp.concatenate([v1[a:a + 1] + v2 for a in range(PEER_TOPK)], axis=0)
        cidx = jnp.concatenate([i1[a:a + 1] * PEER_KEYS + i2 for a in range(PEER_TOPK)], axis=0)
        top_s, expert = _topk_rows(cand, cidx)
        ex = jnp.exp(top_s - top_s[0:1])
        gate_rows.append(ex / jnp.sum(ex, axis=0, keepdims=True))
        idx_rows.append(expert)
    idx_t = jnp.concatenate(idx_rows, axis=0)
    gate_t = jnp.concatenate(gate_rows, axis=0)
    idx_ref[...] = idx_t.T
    for c in range(tb // GATHER_BLOCK):
        gate_ref[c] = gate_t[:, c * GATHER_BLOCK:(c + 1) * GATHER_BLOCK]


def _route(h2d, wq, k1, k2):
    n, d = h2d.shape
    qd = wq.shape[1]
    tb = ROUTE_BLOCK
    est = 2 * tb * d * 4 + d * qd * 2 + tb * qd * 8 + 64 * tb * 4 * 64
    return pl.pallas_call(
        _route_kernel,
        grid=(n // tb,),
        in_specs=[pl.BlockSpec((tb, d), lambda i: (i, 0)),
                  pl.BlockSpec((d, qd), lambda i: (0, 0)),
                  pl.BlockSpec((PEER_HEADS, PEER_KEYS, PEER_HALF), lambda i: (0, 0, 0)),
                  pl.BlockSpec((PEER_HEADS, PEER_KEYS, PEER_HALF), lambda i: (0, 0, 0))],
        out_specs=[pl.BlockSpec((tb, PEER_SLOTS), lambda i: (i, 0)),
                   pl.BlockSpec((tb // GATHER_BLOCK, PEER_SLOTS, GATHER_BLOCK), lambda i: (i, 0, 0))],
        out_shape=[jax.ShapeDtypeStruct((n, PEER_SLOTS), jnp.int32),
                   jax.ShapeDtypeStruct((n // GATHER_BLOCK, PEER_SLOTS, GATHER_BLOCK), F32)],
        compiler_params=pltpu.CompilerParams(dimension_semantics=("parallel",),
                                             vmem_limit_bytes=_vmem_limit(est)),
        name="route",
    )(h2d, wq.astype(BF16), k1.astype(BF16), k2.astype(BF16))


_BFLY_ORDER = (0, 4, 2, 6, 1, 5, 3, 7)


def _hidden_kernel(idx_ref, x_ref, gate_ref, tab_ref, act_ref):
    tb = x_ref.shape[0]
    sub = lax.broadcasted_iota(jnp.int32, (SUBLANES, LANES), 0)
    lane = lax.broadcasted_iota(jnp.int32, (SUBLANES, LANES), 1)
    m4 = sub < 4
    m2 = (sub & 3) < 2
    m1 = (sub & 1) < 1

    def fold4(a, b):
        return jnp.where(m4, a, b) + pltpu.roll(jnp.where(m4, b, a), 4, axis=0)

    def fold(a, b, m, sh):
        return jnp.where(m, a + pltpu.roll(a, SUBLANES - sh, axis=0), b + pltpu.roll(b, sh, axis=0))

    def token(t, accs):
        xt = x_ref[t]
        out = []
        for j in range(PEER_SLOTS // SUBLANES):
            p = [tab_ref[idx_ref[t, j * SUBLANES + _BFLY_ORDER[i]]].astype(F32) * xt for i in range(SUBLANES)]
            a = [fold4(p[0], p[1]), fold4(p[2], p[3]), fold4(p[4], p[5]), fold4(p[6], p[7])]
            b = [fold(a[0], a[1], m2, 2), fold(a[2], a[3], m2, 2)]
            c = fold(b[0], b[1], m1, 1)
            r = jnp.sum(c, axis=1, keepdims=True)
            out.append(jnp.where(lane == t, r, accs[j]))
        return tuple(out)

    zero = jnp.zeros((SUBLANES, LANES), F32)
    accs = lax.fori_loop(0, tb, token, tuple(zero for _ in range(PEER_SLOTS // SUBLANES)))
    hid_t = jnp.concatenate(accs, axis=0)
    act_ref[...] = (gate_ref[...] * jax.nn.gelu(hid_t)).T


def _hidden(idx, x3v, gate_t, tab):
    n = idx.shape[0]
    tb = GATHER_BLOCK
    ne = tab.shape[0]
    est = ne * SUBLANES * LANES * 2 + 4 * tb * SUBLANES * LANES * 4
    return pl.pallas_call(
        _hidden_kernel,
        grid=(n // tb,),
        in_specs=[pl.BlockSpec((tb, PEER_SLOTS), lambda i: (i, 0), memory_space=pltpu.SMEM),
                  pl.BlockSpec((tb, SUBLANES, LANES), lambda i: (i, 0, 0)),
                  pl.BlockSpec((None, PEER_SLOTS, tb), lambda i: (i, 0, 0)),
                  pl.BlockSpec((ne, SUBLANES, LANES), lambda i: (0, 0, 0), pipeline_mode=pl.Buffered(1))],
        out_specs=pl.BlockSpec((tb, PEER_SLOTS), lambda i: (i, 0)),
        out_shape=jax.ShapeDtypeStruct((n, PEER_SLOTS), F32),
        compiler_params=pltpu.CompilerParams(dimension_semantics=("parallel",),
                                             vmem_limit_bytes=_vmem_limit(est)),
        name="hidden",
    )(idx, x3v, gate_t, tab)


def _expert_kernel(idx_ref, act_ref, tab_ref, o_ref):
    tb = o_ref.shape[0]
    nacc = 4

    def token(t, carry):
        accs = [jnp.zeros((SUBLANES, LANES), F32) for _ in range(nacc)]
        for k in range(PEER_SLOTS):
            accs[k % nacc] = accs[k % nacc] + act_ref[t, k] * tab_ref[idx_ref[t, k]].astype(F32)
        o_ref[t] = (accs[0] + accs[1]) + (accs[2] + accs[3])
        return carry

    lax.fori_loop(0, tb, token, 0)


def _expert(idx, act, tab):
    n = idx.shape[0]
    tb = GATHER_BLOCK
    ne = tab.shape[0]
    est = ne * SUBLANES * LANES * 2 + 2 * tb * SUBLANES * LANES * 4
    return pl.pallas_call(
        _expert_kernel,
        grid=(n // tb,),
        in_specs=[pl.BlockSpec((tb, PEER_SLOTS), lambda i: (i, 0), memory_space=pltpu.SMEM),
                  pl.BlockSpec((tb, PEER_SLOTS), lambda i: (i, 0), memory_space=pltpu.SMEM),
                  pl.BlockSpec((ne, SUBLANES, LANES), lambda i: (0, 0, 0), pipeline_mode=pl.Buffered(1))],
        out_specs=pl.BlockSpec((tb, SUBLANES, LANES), lambda i: (i, 0, 0)),
        out_shape=jax.ShapeDtypeStruct((n, SUBLANES, LANES), F32),
        compiler_params=pltpu.CompilerParams(dimension_semantics=("parallel",),
                                             vmem_limit_bytes=_vmem_limit(est)),
        name="expert",
    )(idx, act, tab)


def _final_kernel(alpha, h_ref, f_ref, g_ref, b_ref, o_ref):
    o_ref[...] = _layer_norm(alpha * h_ref[...] + f_ref[...], g_ref[...], b_ref[...])


def _final(h2d, ffn2d, ln_g, ln_b, alpha):
    n, d = h2d.shape
    rb = ROW_BLOCK
    blk = pl.BlockSpec((rb, d), lambda i: (i, 0))
    vec = pl.BlockSpec((1, d), lambda i: (0, 0))
    return pl.pallas_call(
        functools.partial(_final_kernel, alpha),
        grid=(n // rb,),
        in_specs=[blk, blk, vec, vec],
        out_specs=blk,
        out_shape=jax.ShapeDtypeStruct((n, d), F32),
        compiler_params=pltpu.CompilerParams(dimension_semantics=("parallel",),
                                             vmem_limit_bytes=_vmem_limit(6 * rb * d * 4)),
        name="final",
    )(h2d, ffn2d, ln_g.astype(F32).reshape(1, d), ln_b.astype(F32).reshape(1, d))


def _expert_table(tab):
    ne, d = tab.shape
    assert d == SUBLANES * LANES
    return tab.astype(BF16).reshape(ne, SUBLANES, LANES)


def kernel(x, w_in, hgrn_lb_logits, hgrn_norm_g, gmlp_ln_g, gmlp_ln_b, gmlp_ws, gmlp_bs, w_out, ln1_g, ln1_b,
           peer_wq, peer_k1, peer_k2, peer_u, peer_v, ln2_g, ln2_b):
    bsz, s, d = x.shape
    depth = w_in.shape[0]
    n = bsz * s
    assert s % ROW_BLOCK == 0 and n % ROUTE_BLOCK == 0 and d == SUBLANES * LANES
    alpha = float((2 * depth) ** 0.25)
    h = x
    for l in range(depth):
        proj3 = _proj(h.reshape(n, d), w_in[l].astype(BF16)).reshape(bsz, s, -1)
        yhg = _hgrn(proj3, hgrn_lb_logits, hgrn_norm_g[l], l)
        h = _mix(yhg, proj3, h, gmlp_ln_g[l], gmlp_ln_b[l], gmlp_ws[l], gmlp_bs[l], w_out[l],
                 ln1_g[l], ln1_b[l], alpha)
        h2d = h.reshape(n, d)
        idx, gate_t = _route(h2d, peer_wq[l], peer_k1[l], peer_k2[l])
        act = _hidden(idx, h2d.reshape(n, SUBLANES, LANES), gate_t, _expert_table(peer_u[l]))
        ffn = _expert(idx, act, _expert_table(peer_v[l]))
        h = _final(h2d, ffn.reshape(n, d), ln2_g[l], ln2_b[l], alpha).reshape(bsz, s, d)
    return h
```
